```python
import jax, jax.numpy as jnp
from jax import lax
import numpy as np

D_MODEL = 2048
BATCH = 4
SEQ = 4096
DEPTH = 2

CHUNK = 64
QUERY_BLOCK = 128
D_MIX = D_MODEL
SB_WIDTH = D_MIX // 2
SB_HEADS = 8
SB_HEAD_DIM = SB_WIDTH // SB_HEADS
GLA_VALUE_WIDTH = D_MIX - SB_WIDTH
GLA_HEADS = 4
GLA_KEY_WIDTH = GLA_VALUE_WIDTH // 2
GLA_HEAD_K = GLA_KEY_WIDTH // GLA_HEADS
GLA_HEAD_V = GLA_VALUE_WIDTH // GLA_HEADS
GLA_GATE_RANK = 16
GLA_GATE_TAU = 16.0
NORM_EPS = 1e-6
SPLIT_SIZES = (SB_WIDTH, SB_WIDTH, SB_WIDTH, SB_WIDTH,
               GLA_KEY_WIDTH, GLA_KEY_WIDTH, GLA_VALUE_WIDTH,
               GLA_VALUE_WIDTH, GLA_GATE_RANK)
IN_WIDTH = sum(SPLIT_SIZES)

kernel_name = "hybrid_stickbreaking_gla_block"


def rms_norm(x, gain):
    xf = x.astype(jnp.float32)
    y = xf * lax.rsqrt(jnp.mean(xf * xf, axis=-1, keepdims=True) + NORM_EPS)
    return (y * gain.astype(jnp.float32)).astype(x.dtype)


def to_heads(t, n_heads):
    b, s, w = t.shape
    return t.reshape(b, s, n_heads, w // n_heads).transpose(0, 2, 1, 3)


def from_heads(t):
    b, h, s, d = t.shape
    return t.transpose(0, 2, 1, 3).reshape(b, s, h * d)


def stick_breaking_attention(q, k, v):
    _, _, s_len, d = q.shape
    scale = d ** -0.5
    outs = []
    for i in range(s_len // QUERY_BLOCK):
        start = i * QUERY_BLOCK
        end = start + QUERY_BLOCK
        qb = q[:, :, start:end].astype(jnp.float32)
        kb = k[:, :, :end].astype(jnp.float32)
        vb = v[:, :, :end].astype(jnp.float32)
        z = jnp.einsum('bhqd,bhkd->bhqk', qb, kb) * scale
        t_idx = start + jnp.arange(QUERY_BLOCK)[:, None]
        s_idx = jnp.arange(end)[None, :]
        mask = s_idx < t_idx
        log_fail = jnp.where(mask, jax.nn.log_sigmoid(-z), 0.0)
        later = lax.cumsum(log_fail, axis=3, reverse=True) - log_fail
        w = jnp.where(mask, jnp.exp(jax.nn.log_sigmoid(z) + later), 0.0)
        outs.append(jnp.einsum('bhqk,bhkd->bhqd', w, vb))
    return jnp.concatenate(outs, axis=2).astype(v.dtype)


def gated_linear_attention(q, k, v, log_alpha):
    b, h, s_len, dk = q.shape
    dv = v.shape[-1]
    n_c = s_len // CHUNK
    qf = (q.astype(jnp.float32) * dk ** -0.5).reshape(b, h, n_c, CHUNK, dk)
    kf = k.astype(jnp.float32).reshape(b, h, n_c, CHUNK, dk)
    vf = v.astype(jnp.float32).reshape(b, h, n_c, CHUNK, dv)
    cum = jnp.cumsum(log_alpha.astype(jnp.float32).reshape(b, h, n_c, CHUNK, dk), axis=3)
    q_dec = qf * jnp.exp(cum)
    k_inv = kf * jnp.exp(-cum)
    scores = jnp.einsum('bhncd,bhned->bhnce', q_dec, k_inv)
    causal = jnp.arange(CHUNK)[None, :] <= jnp.arange(CHUNK)[:, None]
    scores = jnp.where(causal, scores, 0.0)
    o_intra = jnp.einsum('bhnce,bhnev->bhncv', scores, vf)
    cum_last = cum[:, :, :, -1:, :]
    kv = jnp.einsum('bhncd,bhncv->bhndv', kf * jnp.exp(cum_last - cum), vf)
    chunk_decay = jnp.exp(cum_last[:, :, :, 0, :])

    def step(state, inp):
        dec, kv_c = inp
        return dec[..., None] * state + kv_c, state

    init = jnp.zeros((b, h, dk, dv), jnp.float32)
    _, states = lax.scan(step, init, (jnp.moveaxis(chunk_decay, 2, 0), jnp.moveaxis(kv, 2, 0)))
    states = jnp.moveaxis(states, 0, 2)
    o_inter = jnp.einsum('bhncd,bhndv->bhncv', q_dec, states)
    return (o_intra + o_inter).reshape(b, h, s_len, dv).astype(v.dtype)


def setup_inputs(seed: int = 0) -> dict:
    key = jax.random.key(seed)
    ks = jax.random.split(key, 11)
    f32 = jnp.float32
    x = jax.random.normal(ks[0], (BATCH, SEQ, D_MODEL), f32)
    norm_gain = 1.0 + 0.02 * jax.random.normal(ks[1], (DEPTH, D_MODEL), f32)
    w_in = jax.random.normal(ks[2], (DEPTH, D_MODEL, IN_WIDTH), f32) * D_MODEL ** -0.5
    sb_q_gain = 1.0 + 0.02 * jax.random.normal(ks[3], (DEPTH, SB_HEAD_DIM), f32)
    sb_k_gain = 1.0 + 0.02 * jax.random.normal(ks[4], (DEPTH, SB_HEAD_DIM), f32)
    sb_o_gain = 1.0 + 0.02 * jax.random.normal(ks[5], (DEPTH, SB_WIDTH), f32)
    gla_w_alpha = jax.random.normal(ks[6], (DEPTH, GLA_GATE_RANK, GLA_KEY_WIDTH), f32) * GLA_GATE_RANK ** -0.5
    gla_b_alpha = 0.1 * jax.random.normal(ks[7], (DEPTH, GLA_KEY_WIDTH), f32)
    gla_o_gain = 1.0 + 0.02 * jax.random.normal(ks[8], (DEPTH, GLA_VALUE_WIDTH), f32)
    w_out = jax.random.normal(ks[9], (DEPTH, D_MIX, D_MODEL), f32) * (0.5 * D_MIX ** -0.5)
    return {"x": x, "norm_gain": norm_gain, "w_in": w_in, "sb_q_gain": sb_q_gain,
            "sb_k_gain": sb_k_gain, "sb_o_gain": sb_o_gain, "gla_w_alpha": gla_w_alpha,
            "gla_b_alpha": gla_b_alpha, "gla_o_gain": gla_o_gain, "w_out": w_out}


def reference(x, norm_gain, w_in, sb_q_gain, sb_k_gain, sb_o_gain, gla_w_alpha,
              gla_b_alpha, gla_o_gain, w_out):
    offsets = np.cumsum(SPLIT_SIZES)[:-1].tolist()
    for layer in range(DEPTH):
        h = rms_norm(x, norm_gain[layer])
        proj = jnp.einsum('bsd,de->bse', h, w_in[layer])
        (sb_q, sb_k, sb_v, sb_g, gl_q, gl_k, gl_v, gl_g, gl_lr) = jnp.split(proj, offsets, axis=-1)

        q = rms_norm(to_heads(sb_q, SB_HEADS), sb_q_gain[layer])
        k = rms_norm(to_heads(sb_k, SB_HEADS), sb_k_gain[layer])
        v = to_heads(sb_v, SB_HEADS)
        a_out = stick_breaking_attention(q, k, v)
        a_out = rms_norm(a_out, sb_o_gain[layer].reshape(SB_HEADS, 1, SB_HEAD_DIM))
        a_out = from_heads(a_out) * jax.nn.silu(sb_g)

        gate_logits = jnp.einsum('bsr,rk->bsk', gl_lr, gla_w_alpha[layer]) + gla_b_alpha[layer]
        log_alpha = jax.nn.log_sigmoid(gate_logits.astype(jnp.float32)) / GLA_GATE_TAU
        b_out = gated_linear_attention(to_heads(gl_q, GLA_HEADS), to_heads(gl_k, GLA_HEADS),
                                       to_heads(gl_v, GLA_HEADS), to_heads(log_alpha, GLA_HEADS))
        b_out = rms_norm(b_out, gla_o_gain[layer].reshape(GLA_HEADS, 1, GLA_HEAD_V))
        b_out = from_heads(b_out) * jax.nn.silu(gl_g)

        mixed = jnp.concatenate([a_out, b_out], axis=-1)
        x = x + jnp.einsum('bse,ed->bsd', mixed, w_out[layer]).astype(x.dtype)
    return x
```

```python
import functools
import math

import jax
import jax.numpy as jnp
from jax import lax
from jax.experimental import pallas as pl
from jax.experimental.pallas import tpu as pltpu

F32 = jnp.float32
BF16 = jnp.bfloat16

D_MODEL = 2048
SB_WIDTH = 1024
SB_HEADS = 8
SB_HEAD_DIM = 128
GLA_HEADS = 4
GLA_KEY_WIDTH = 512
GLA_VALUE_WIDTH = 1024
GLA_HEAD_K = 128
GLA_HEAD_V = 256
GLA_GATE_RANK = 16
GLA_GATE_TAU = 16.0
GLA_CHUNK = 64
NORM_EPS = 1e-6
MAIN_WIDTH = 4 * SB_WIDTH + 2 * GLA_KEY_WIDTH + 2 * GLA_VALUE_WIDTH

LANES = 128
LOG2E = math.log2(math.e)

SB_Q_BLK, SB_K_BLK, SB_V_BLK, SB_G_BLK = 0, 8, 16, 24
GLA_Q_BLK, GLA_K_BLK = 32, 36
GLA_V_BLK256, GLA_G_BLK256 = 20, 24

IN_TM, IN_TN, IN_RB = 1024, 512, 128
IN_QK_BLOCKS = 2 * SB_WIDTH // IN_TN
SB_TQ = 256
GLA_T = 512
OUT_TM = 512


def _inproj_kernel(x_ref, ng_ref, w_ref, wlr_ref, hg_ref, o_ref, lr_ref, h_scr):
    j = pl.program_id(1)

    @pl.when(j == 0)
    def _():
        def rows(r, c):
            sl = pl.ds(pl.multiple_of(r * IN_RB, IN_RB), IN_RB)
            x = x_ref[sl, :]
            ms = jnp.mean(x * x, axis=-1, keepdims=True)
            h_scr[sl, :] = (x * lax.rsqrt(ms + NORM_EPS) * ng_ref[...]).astype(BF16)
            return c

        lax.fori_loop(0, IN_TM // IN_RB, rows, 0)
        lr_ref[...] = jnp.dot(h_scr[...], wlr_ref[...], preferred_element_type=F32)

    acc = jnp.dot(h_scr[...], w_ref[...], preferred_element_type=F32)

    @pl.when(j < IN_QK_BLOCKS)
    def _():
        for hh in range(IN_TN // SB_HEAD_DIM):
            cs = slice(hh * SB_HEAD_DIM, (hh + 1) * SB_HEAD_DIM)
            a = acc[:, cs]
            ms = jnp.mean(a * a, axis=-1, keepdims=True)
            o_ref[:, cs] = (a * lax.rsqrt(ms + NORM_EPS) * hg_ref[:, cs]).astype(BF16)

    @pl.when(j >= IN_QK_BLOCKS)
    def _():
        o_ref[...] = acc.astype(BF16)


def _inproj(xf, ng, w_main, w_lr, hg):
    m = xf.shape[0]
    return pl.pallas_call(
        _inproj_kernel,
        grid=(m // IN_TM, MAIN_WIDTH // IN_TN),
        in_specs=[
            pl.BlockSpec((IN_TM, D_MODEL), lambda i, j: (i, 0)),
            pl.BlockSpec((1, D_MODEL), lambda i, j: (0, 0)),
            pl.BlockSpec((D_MODEL, IN_TN), lambda i, j: (0, j)),
            pl.BlockSpec((D_MODEL, LANES), lambda i, j: (0, 0)),
            pl.BlockSpec((1, IN_TN), lambda i, j: (0, jnp.minimum(j, IN_QK_BLOCKS - 1))),
        ],
        out_specs=[
            pl.BlockSpec((IN_TM, IN_TN), lambda i, j: (i, j)),
            pl.BlockSpec((IN_TM, LANES), lambda i, j: (i, 0)),
        ],
        out_shape=[
            jax.ShapeDtypeStruct((m, MAIN_WIDTH), BF16),
            jax.ShapeDtypeStruct((m, LANES), F32),
        ],
        scratch_shapes=[pltpu.VMEM((IN_TM, D_MODEL), BF16)],
        compiler_params=pltpu.CompilerParams(dimension_semantics=("arbitrary", "arbitrary")),
        name="inproj",
    )(xf, ng, w_main, w_lr, hg)


def _sb_kernel(q_ref, k_ref, v_ref, g_ref, og_ref, u_ref, o_ref, acc_ref, l_ref):
    i = pl.program_id(2)
    q = q_ref[0]
    u = u_ref[...]
    acc_ref[...] = jnp.zeros_like(acc_ref)
    l_ref[...] = jnp.zeros_like(l_ref)

    def block(off, mask):
        kj = k_ref[0, pl.ds(off, SB_TQ), :]
        vj = v_ref[0, pl.ds(off, SB_TQ), :]
        z = lax.dot_general(q, kj, (((1,), (1,)), ((), ())), preferred_element_type=F32)
        e = jnp.exp2(-jnp.abs(z))
        sp = jnp.maximum(z, 0.0) + jnp.log(1.0 + e) * LOG2E
        spm = sp if mask is None else jnp.where(mask, sp, 0.0)
        hi = spm.astype(BF16)
        lo = (spm - hi.astype(F32)).astype(BF16)
        cum = jnp.dot(jnp.concatenate([hi, lo], axis=1), u, preferred_element_type=F32)
        w = jnp.exp2(z - sp + cum + l_ref[...])
        if mask is not None:
            w = jnp.where(mask, w, 0.0)
        acc_ref[...] += jnp.dot(w.astype(BF16), vj, preferred_element_type=F32)
        l_ref[...] += cum[:, :1] - spm[:, :1]

    row = lax.broadcasted_iota(jnp.int32, (SB_TQ, SB_TQ), 0)
    col = lax.broadcasted_iota(jnp.int32, (SB_TQ, SB_TQ), 1)
    block(pl.multiple_of(i * SB_TQ, SB_TQ), col < row)

    def body(jj, c):
        block(pl.multiple_of((i - 1 - jj) * SB_TQ, SB_TQ), None)
        return c

    lax.fori_loop(0, i, body, 0)

    o = acc_ref[...]
    ms = jnp.mean(o * o, axis=-1, keepdims=True)
    y = o * lax.rsqrt(ms + NORM_EPS) * og_ref[...]
    g = g_ref[0].astype(F32)
    o_ref[0] = (y * (g / (1.0 + jnp.exp(-g)))).astype(BF16)


def _sb_attention(proj, og, u):
    b, s, _ = proj.shape
    return pl.pallas_call(
        _sb_kernel,
        grid=(b, SB_HEADS, s // SB_TQ),
        in_specs=[
            pl.BlockSpec((1, SB_TQ, SB_HEAD_DIM), lambda bb, h, i: (bb, i, SB_Q_BLK + h)),
            pl.BlockSpec((1, s, SB_HEAD_DIM), lambda bb, h, i: (bb, 0, SB_K_BLK + h)),
            pl.BlockSpec((1, s, SB_HEAD_DIM), lambda bb, h, i: (bb, 0, SB_V_BLK + h)),
            pl.BlockSpec((1, SB_TQ, SB_HEAD_DIM), lambda bb, h, i: (bb, i, SB_G_BLK + h)),
            pl.BlockSpec((1, SB_HEAD_DIM), lambda bb, h, i: (0, h)),
            pl.BlockSpec((2 * SB_TQ, SB_TQ), lambda bb, h, i: (0, 0)),
        ],
        out_specs=pl.BlockSpec((1, SB_TQ, SB_HEAD_DIM), lambda bb, h, i: (bb, i, h)),
        out_shape=jax.ShapeDtypeStruct((b, s, SB_WIDTH), BF16),
        scratch_shapes=[pltpu.VMEM((SB_TQ, SB_HEAD_DIM), F32), pltpu.VMEM((SB_TQ, 1), F32)],
        compiler_params=pltpu.CompilerParams(dimension_semantics=("arbitrary", "arbitrary", "arbitrary")),
        name="sb_attention",
    )(proj, proj, proj, proj, og, u)


def _gla_kernel(q_ref, k_ref, v_ref, g_ref, lr_ref, wa_ref, ba_ref, og_ref, o_ref, st_ref):
    t = pl.program_id(2)
    c_len = GLA_CHUNK

    @pl.when(t == 0)
    def _():
        st_ref[...] = jnp.zeros_like(st_ref)

    logits = jnp.dot(lr_ref[0].astype(BF16), wa_ref[...], preferred_element_type=F32) + ba_ref[...]
    la = (jnp.minimum(logits, 0.0) - jnp.log(1.0 + jnp.exp(-jnp.abs(logits)))) * (1.0 / GLA_GATE_TAU)

    row = lax.broadcasted_iota(jnp.int32, (c_len, c_len), 0)
    col = lax.broadcasted_iota(jnp.int32, (c_len, c_len), 1)
    causal = col <= row
    ltri = jnp.where(causal, 1.0, 0.0).astype(BF16)
    ltri2 = jnp.concatenate([ltri, ltri], axis=1)
    scale = GLA_HEAD_K ** -0.5
    st = st_ref[...]

    for c in range(GLA_T // c_len):
        sl = slice(c * c_len, (c + 1) * c_len)
        la_c = la[sl]
        hi = la_c.astype(BF16)
        lo = (la_c - hi.astype(F32)).astype(BF16)
        cum = jnp.dot(ltri2, jnp.concatenate([hi, lo], axis=0), preferred_element_type=F32)
        cl = cum[c_len - 1:c_len, :]
        qf = q_ref[0, sl, :].astype(F32) * scale
        kf = k_ref[0, sl, :].astype(F32)
        qd = (qf * jnp.exp(cum)).astype(BF16)
        ki = (kf * jnp.exp(-cum)).astype(BF16)
        kl = (kf * jnp.exp(cl - cum)).astype(BF16)
        vb = v_ref[0, sl, :]
        sc = lax.dot_general(qd, ki, (((1,), (1,)), ((), ())), preferred_element_type=F32)
        sc = jnp.where(causal, sc, 0.0).astype(BF16)
        o = jnp.dot(sc, vb, preferred_element_type=F32)
        o = o + lax.dot_general(qd, st.astype(BF16), (((1,), (1,)), ((), ())), preferred_element_type=F32)
        kvt = lax.dot_general(vb, kl, (((0,), (0,)), ((), ())), preferred_element_type=F32)
        st = jnp.exp(cl) * st + kvt
        ms = jnp.mean(o * o, axis=-1, keepdims=True)
        y = o * lax.rsqrt(ms + NORM_EPS) * og_ref[...]
        g = g_ref[0, sl, :].astype(F32)
        o_ref[0, sl, :] = (y * (g / (1.0 + jnp.exp(-g)))).astype(BF16)

    st_ref[...] = st


def _gla(proj, lr, wa, ba, og):
    b, s, _ = proj.shape
    return pl.pallas_call(
        _gla_kernel,
        grid=(b, GLA_HEADS, s // GLA_T),
        in_specs=[
            pl.BlockSpec((1, GLA_T, GLA_HEAD_K), lambda bb, h, t: (bb, t, GLA_Q_BLK + h)),
            pl.BlockSpec((1, GLA_T, GLA_HEAD_K), lambda bb, h, t: (bb, t, GLA_K_BLK + h)),
            pl.BlockSpec((1, GLA_T, GLA_HEAD_V), lambda bb, h, t: (bb, t, GLA_V_BLK256 + h)),
            pl.BlockSpec((1, GLA_T, GLA_HEAD_V), lambda bb, h, t: (bb, t, GLA_G_BLK256 + h)),
            pl.BlockSpec((1, GLA_T, LANES), lambda bb, h, t: (bb, t, 0)),
            pl.BlockSpec((LANES, GLA_HEAD_K), lambda bb, h, t: (0, h)),
            pl.BlockSpec((1, GLA_HEAD_K), lambda bb, h, t: (0, h)),
            pl.BlockSpec((1, GLA_HEAD_V), lambda bb, h, t: (0, h)),
        ],
        out_specs=pl.BlockSpec((1, GLA_T, GLA_HEAD_V), lambda bb, h, t: (bb, t, h)),
        out_shape=jax.ShapeDtypeStruct((b, s, GLA_VALUE_WIDTH), BF16),
        scratch_shapes=[pltpu.VMEM((GLA_HEAD_V, GLA_HEAD_K), F32)],
        compiler_params=pltpu.CompilerParams(dimension_semantics=("arbitrary", "arbitrary", "arbitrary")),
        name="gla",
    )(proj, proj, proj, proj, lr, wa, ba, og)


def _outproj_kernel(a_ref, b_ref, x_ref, w_ref, o_ref):
    y = jnp.dot(a_ref[...], w_ref[:SB_WIDTH, :], preferred_element_type=F32)
    y = y + jnp.dot(b_ref[...], w_ref[SB_WIDTH:, :], preferred_element_type=F32)
    o_ref[...] = x_ref[...] + y


def _outproj(a, b, xf, w):
    m = xf.shape[0]
    return pl.pallas_call(
        _outproj_kernel,
        grid=(m // OUT_TM,),
        in_specs=[
            pl.BlockSpec((OUT_TM, SB_WIDTH), lambda i: (i, 0)),
            pl.BlockSpec((OUT_TM, GLA_VALUE_WIDTH), lambda i: (i, 0)),
            pl.BlockSpec((OUT_TM, D_MODEL), lambda i: (i, 0)),
            pl.BlockSpec((SB_WIDTH + GLA_VALUE_WIDTH, D_MODEL), lambda i: (0, 0)),
        ],
        out_specs=pl.BlockSpec((OUT_TM, D_MODEL), lambda i: (i, 0)),
        out_shape=jax.ShapeDtypeStruct((m, D_MODEL), F32),
        compiler_params=pltpu.CompilerParams(dimension_semantics=("arbitrary",)),
        name="outproj",
    )(a, b, xf, w)


def _suffix_sum_matrix():
    r = lax.broadcasted_iota(jnp.int32, (SB_TQ, SB_TQ), 0)
    c = lax.broadcasted_iota(jnp.int32, (SB_TQ, SB_TQ), 1)
    u = jnp.where(r > c, -1.0, 0.0).astype(BF16)
    return jnp.concatenate([u, u], axis=0)


@jax.jit
def kernel(x, norm_gain, w_in, sb_q_gain, sb_k_gain, sb_o_gain, gla_w_alpha, gla_b_alpha, gla_o_gain, w_out):
    b, s, d = x.shape
    depth = w_in.shape[0]
    assert d == D_MODEL and w_in.shape[2] == MAIN_WIDTH + GLA_GATE_RANK
    assert (b * s) % IN_TM == 0 and s % GLA_T == 0 and s % SB_TQ == 0

    xf = x.reshape(b * s, d)
    u = _suffix_sum_matrix()
    q_scale = SB_HEAD_DIM ** -0.5 * LOG2E
    for layer in range(depth):
        w = w_in[layer]
        w_main = w[:, :MAIN_WIDTH].astype(BF16)
        w_lr = jnp.pad(w[:, MAIN_WIDTH:], ((0, 0), (0, LANES - GLA_GATE_RANK))).astype(BF16)
        hg = jnp.concatenate([jnp.tile(sb_q_gain[layer] * q_scale, SB_HEADS),
                              jnp.tile(sb_k_gain[layer], SB_HEADS)]).reshape(1, 2 * SB_WIDTH)
        proj, lr = _inproj(xf, norm_gain[layer].reshape(1, d), w_main, w_lr, hg)
        proj = proj.reshape(b, s, MAIN_WIDTH)
        lr = lr.reshape(b, s, LANES)

        mixed_a = _sb_attention(proj, sb_o_gain[layer].reshape(1, SB_WIDTH), u)

        wa = jnp.pad(gla_w_alpha[layer], ((0, LANES - GLA_GATE_RANK), (0, 0))).astype(BF16)
        mixed_b = _gla(proj, lr, wa, gla_b_alpha[layer].reshape(1, GLA_KEY_WIDTH),
                       gla_o_gain[layer].reshape(1, GLA_VALUE_WIDTH))

        xf = _outproj(mixed_a.reshape(b * s, SB_WIDTH), mixed_b.reshape(b * s, GLA_VALUE_WIDTH),
                      xf, w_out[layer].astype(BF16))
    return xf.reshape(b, s, d)
```

```python
import math

import jax
import jax.numpy as jnp
from jax import lax
from jax.experimental import pallas as pl
from jax.experimental.pallas import tpu as pltpu

F32 = jnp.float32
BF16 = jnp.bfloat16

D_MODEL = 2048
SB_WIDTH = 1024
SB_HEADS = 8
SB_HEAD_DIM = 128
GLA_HEADS = 4
GLA_KEY_WIDTH = 512
GLA_VALUE_WIDTH = 1024
GLA_HEAD_K = 128
GLA_HEAD_V = 256
GLA_GATE_RANK = 16
GLA_GATE_TAU = 16.0
GLA_CHUNK = 64
NORM_EPS = 1e-6
MAIN_WIDTH = 4 * SB_WIDTH + 2 * GLA_KEY_WIDTH + 2 * GLA_VALUE_WIDTH

LANES = 128
N_SLABS = MAIN_WIDTH // LANES
LOG2E = math.log2(math.e)
MASK_BIAS = -1e30

SB_Q_SLAB, SB_K_SLAB, SB_V_SLAB, SB_G_SLAB = 0, 8, 16, 24
GLA_Q_SLAB, GLA_K_SLAB, GLA_V_SLAB, GLA_G_SLAB = 32, 36, 40, 48

IN_TM, IN_TN, IN_RB = 1024, 512, 128
IN_SLABS = IN_TN // LANES
IN_QK_BLOCKS = 2 * SB_WIDTH // IN_TN
SB_TQ = 256
SB_HP = 4
SB_QB = 4
SB_TBL_JJ = 16
SB_TBL_SHIFT = SB_TBL_JJ.bit_length() - 1
GLA_T = 512
OUT_TM = 512


def _inproj_kernel(x_ref, ng_ref, w_ref, wlr_ref, hg_ref, o_ref, lr_ref, h_scr):
    j = pl.program_id(1)

    @pl.when(j == 0)
    def _():
        def rows(r, c):
            sl = pl.ds(pl.multiple_of(r * IN_RB, IN_RB), IN_RB)
            x = x_ref[sl, :]
            ms = jnp.mean(x * x, axis=-1, keepdims=True)
            h_scr[sl, :] = (x * lax.rsqrt(ms + NORM_EPS) * ng_ref[...]).astype(BF16)
            return c

        lax.fori_loop(0, IN_TM // IN_RB, rows, 0)
        lr_ref[...] = jnp.dot(h_scr[...], wlr_ref[...], preferred_element_type=F32)

    acc = jnp.dot(h_scr[...], w_ref[...], preferred_element_type=F32)

    @pl.when(j < IN_QK_BLOCKS)
    def _():
        for hh in range(IN_SLABS):
            cs = slice(hh * LANES, (hh + 1) * LANES)
            a = acc[:, cs]
            ms = jnp.mean(a * a, axis=-1, keepdims=True)
            o_ref[0, hh] = (a * lax.rsqrt(ms + NORM_EPS) * hg_ref[:, cs]).astype(BF16)

    @pl.when(j >= IN_QK_BLOCKS)
    def _():
        for hh in range(IN_SLABS):
            o_ref[0, hh] = acc[:, hh * LANES:(hh + 1) * LANES].astype(BF16)


def _inproj(xf, ng, w_main, w_lr, hg, b, s):
    m = xf.shape[0]
    per_b = s // IN_TM
    return pl.pallas_call(
        _inproj_kernel,
        grid=(m // IN_TM, MAIN_WIDTH // IN_TN),
        in_specs=[
            pl.BlockSpec((IN_TM, D_MODEL), lambda i, j: (i, 0)),
            pl.BlockSpec((1, D_MODEL), lambda i, j: (0, 0)),
            pl.BlockSpec((D_MODEL, IN_TN), lambda i, j: (0, j)),
            pl.BlockSpec((D_MODEL, LANES), lambda i, j: (0, 0)),
            pl.BlockSpec((1, IN_TN), lambda i, j: (0, jnp.minimum(j, IN_QK_BLOCKS - 1))),
        ],
        out_specs=[
            pl.BlockSpec((1, IN_SLABS, IN_TM, LANES), lambda i, j: (i // per_b, j, i % per_b, 0)),
            pl.BlockSpec((IN_TM, LANES), lambda i, j: (i, 0)),
        ],
        out_shape=[
            jax.ShapeDtypeStruct((b, N_SLABS, s, LANES), BF16),
            jax.ShapeDtypeStruct((m, LANES), F32),
        ],
        scratch_shapes=[pltpu.VMEM((IN_TM, D_MODEL), BF16)],
        compiler_params=pltpu.CompilerParams(dimension_semantics=("arbitrary", "arbitrary")),
        name="inproj",
    )(xf, ng, w_main, w_lr, hg)


def _neg_abs(z):
    zi = lax.bitcast_convert_type(z, jnp.uint32) | jnp.uint32(0x80000000)
    return lax.bitcast_convert_type(zi, F32)


def _sb_kernel(tbl_ref, q_ref, k_ref, v_ref, g_ref, og_ref, u_ref, bias_ref, o_ref,
               acc_ref, l_ref, zs_buf, cum_buf, s0_buf):
    g = pl.program_id(2)
    first_qb = g * SB_QB
    n_pairs = SB_QB * first_qb + (SB_QB * (SB_QB + 1)) // 2
    spare = SB_QB * SB_HP

    @pl.when((pl.program_id(0) == 0) & (pl.program_id(1) == 0) & (g == 0))
    def _():
        zs_buf[...] = jnp.zeros_like(zs_buf)
        cum_buf[...] = jnp.zeros_like(cum_buf)
        s0_buf[...] = jnp.zeros_like(s0_buf)

    acc_ref[...] = jnp.zeros_like(acc_ref)
    l_ref[...] = jnp.zeros_like(l_ref)
    u = u_ref[...]

    def pair(t):
        e = tbl_ref[g, jnp.clip(t, 0, n_pairs - 1)]
        qb = e >> SB_TBL_SHIFT
        jj = e & (SB_TBL_JJ - 1)
        key_off = pl.multiple_of((first_qb + qb - jj) * SB_TQ, SB_TQ)
        return qb, jj, key_off

    def step(t, slot_a):
        slot_b = 1 - slot_a

        qb3, _, key_off3 = pair(t - 1)
        valid3 = (t >= 1) & (t - 1 < n_pairs)
        for hp in range(SB_HP):
            idx = jnp.where(valid3, qb3 * SB_HP + hp, spare)
            cum = cum_buf[slot_b, hp]
            l = l_ref[idx]
            w = jnp.exp2(zs_buf[slot_b, hp] + cum + jnp.concatenate([l, l], axis=1)).astype(BF16)
            vj = v_ref[0, hp, pl.ds(key_off3, SB_TQ), :]
            acc_ref[idx] += jnp.dot(w, vj, preferred_element_type=F32)
            l_ref[idx] = l + jnp.broadcast_to(cum[:, :1] - s0_buf[slot_b, hp], l.shape)

        qb1, jj1, key_off1 = pair(t)
        q_off = pl.multiple_of(qb1 * SB_TQ, SB_TQ)
        bias = bias_ref[jnp.where(jj1 == 0, 1, 0)]
        for hp in range(SB_HP):
            z = lax.dot_general(
                q_ref[0, hp, pl.ds(q_off, SB_TQ), :], k_ref[0, hp, pl.ds(key_off1, SB_TQ), :],
                (((1,), (1,)), ((), ())), preferred_element_type=F32) + bias
            e = jnp.exp2(_neg_abs(z))
            sp = jnp.maximum(z, 0.0) + jnp.log(1.0 + e) * LOG2E
            zs_buf[slot_a, hp] = z - sp
            cum_buf[slot_a, hp] = jnp.dot(sp.astype(BF16), u, preferred_element_type=F32)
            s0_buf[slot_a, hp] = sp[:, :1]

    def body(m, c):
        step(2 * m, 0)
        step(2 * m + 1, 1)
        return c

    lax.fori_loop(0, (n_pairs + 2) // 2, body, 0)

    for qb in range(SB_QB):
        rows = slice(qb * SB_TQ, (qb + 1) * SB_TQ)
        for hp in range(SB_HP):
            o = acc_ref[qb * SB_HP + hp]
            ms = jnp.mean(o * o, axis=-1, keepdims=True)
            y = o * lax.rsqrt(ms + NORM_EPS) * og_ref[hp]
            gate = g_ref[0, hp, rows, :].astype(F32)
            o_ref[0, rows, hp * LANES:(hp + 1) * LANES] = (y * (gate / (1.0 + jnp.exp(-gate)))).astype(BF16)


def _sb_attention(proj, og, u, bias, tbl):
    b, _, s, _ = proj.shape
    tile = (SB_TQ, SB_TQ)
    rows = SB_QB * SB_TQ
    return pl.pallas_call(
        _sb_kernel,
        grid=(b, SB_HEADS // SB_HP, s // rows),
        in_specs=[
            pl.BlockSpec(memory_space=pltpu.SMEM),
            pl.BlockSpec((1, SB_HP, rows, LANES), lambda bb, h, i: (bb, SB_Q_SLAB // SB_HP + h, i, 0)),
            pl.BlockSpec((1, SB_HP, s, LANES), lambda bb, h, i: (bb, SB_K_SLAB // SB_HP + h, 0, 0)),
            pl.BlockSpec((1, SB_HP, s, LANES), lambda bb, h, i: (bb, SB_V_SLAB // SB_HP + h, 0, 0)),
            pl.BlockSpec((1, SB_HP, rows, LANES), lambda bb, h, i: (bb, SB_G_SLAB // SB_HP + h, i, 0)),
            pl.BlockSpec((SB_HP, 1, LANES), lambda bb, h, i: (h, 0, 0)),
            pl.BlockSpec(tile, lambda bb, h, i: (0, 0)),
            pl.BlockSpec((2,) + tile, lambda bb, h, i: (0, 0, 0)),
        ],
        out_specs=pl.BlockSpec((1, rows, SB_HP * LANES), lambda bb, h, i: (bb, i, h)),
        out_shape=jax.ShapeDtypeStruct((b, s, SB_WIDTH), BF16),
        scratch_shapes=[
            pltpu.VMEM((SB_QB * SB_HP + 1, SB_TQ, SB_HEAD_DIM), F32),
            pltpu.VMEM((SB_QB * SB_HP + 1, SB_TQ, LANES), F32),
            pltpu.VMEM((2, SB_HP) + tile, F32),
            pltpu.VMEM((2, SB_HP) + tile, F32),
            pltpu.VMEM((2, SB_HP, SB_TQ, 1), F32),
        ],
        compiler_params=pltpu.CompilerParams(dimension_semantics=("arbitrary", "arbitrary", "arbitrary")),
        name="sb_attention",
    )(tbl, proj, proj, proj, proj, og, u, bias)


def _gla_kernel(q_ref, k_ref, v_ref, g_ref, lr_ref, wa_ref, ba_ref, og_ref, o_ref, st_ref):
    t = pl.program_id(2)
    c_len = GLA_CHUNK

    @pl.when(t == 0)
    def _():
        st_ref[...] = jnp.zeros_like(st_ref)

    logits = jnp.dot(lr_ref[0].astype(BF16), wa_ref[...], preferred_element_type=F32) + ba_ref[...]
    la = (jnp.minimum(logits, 0.0) - jnp.log(1.0 + jnp.exp(-jnp.abs(logits)))) * (1.0 / GLA_GATE_TAU)

    row = lax.broadcasted_iota(jnp.int32, (c_len, c_len), 0)
    col = lax.broadcasted_iota(jnp.int32, (c_len, c_len), 1)
    causal = col <= row
    ltri = jnp.where(causal, 1.0, 0.0).astype(BF16)
    ltri2 = jnp.concatenate([ltri, ltri], axis=1)
    scale = GLA_HEAD_K ** -0.5
    st = st_ref[...]

    for c in range(GLA_T // c_len):
        sl = slice(c * c_len, (c + 1) * c_len)
        la_c = la[sl]
        hi = la_c.astype(BF16)
        lo = (la_c - hi.astype(F32)).astype(BF16)
        cum = jnp.dot(ltri2, jnp.concatenate([hi, lo], axis=0), preferred_element_type=F32)
        cl = cum[c_len - 1:c_len, :]
        qf = q_ref[0, 0, sl, :].astype(F32) * scale
        kf = k_ref[0, 0, sl, :].astype(F32)
        qd = (qf * jnp.exp(cum)).astype(BF16)
        ki = (kf * jnp.exp(-cum)).astype(BF16)
        kl = (kf * jnp.exp(cl - cum)).astype(BF16)
        vb = jnp.concatenate([v_ref[0, 0, sl, :], v_ref[0, 1, sl, :]], axis=1)
        sc = lax.dot_general(qd, ki, (((1,), (1,)), ((), ())), preferred_element_type=F32)
        sc = jnp.where(causal, sc, 0.0).astype(BF16)
        o = jnp.dot(sc, vb, preferred_element_type=F32)
        o = o + lax.dot_general(qd, st.astype(BF16), (((1,), (1,)), ((), ())), preferred_element_type=F32)
        kvt = lax.dot_general(vb, kl, (((0,), (0,)), ((), ())), preferred_element_type=F32)
        st = jnp.exp(cl) * st + kvt
        ms = jnp.mean(o * o, axis=-1, keepdims=True)
        y = o * lax.rsqrt(ms + NORM_EPS) * og_ref[...]
        g = jnp.concatenate([g_ref[0, 0, sl, :], g_ref[0, 1, sl, :]], axis=1).astype(F32)
        o_ref[0, sl, :] = (y * (g / (1.0 + jnp.exp(-g)))).astype(BF16)

    st_ref[...] = st


def _gla(proj, lr, wa, ba, og):
    b, _, s, _ = proj.shape
    return pl.pallas_call(
        _gla_kernel,
        grid=(b, GLA_HEADS, s // GLA_T),
        in_specs=[
            pl.BlockSpec((1, 1, GLA_T, LANES), lambda bb, h, t: (bb, GLA_Q_SLAB + h, t, 0)),
            pl.BlockSpec((1, 1, GLA_T, LANES), lambda bb, h, t: (bb, GLA_K_SLAB + h, t, 0)),
            pl.BlockSpec((1, 2, GLA_T, LANES), lambda bb, h, t: (bb, GLA_V_SLAB // 2 + h, t, 0)),
            pl.BlockSpec((1, 2, GLA_T, LANES), lambda bb, h, t: (bb, GLA_G_SLAB // 2 + h, t, 0)),
            pl.BlockSpec((1, GLA_T, LANES), lambda bb, h, t: (bb, t, 0)),
            pl.BlockSpec((LANES, GLA_HEAD_K), lambda bb, h, t: (0, h)),
            pl.BlockSpec((1, GLA_HEAD_K), lambda bb, h, t: (0, h)),
            pl.BlockSpec((1, GLA_HEAD_V), lambda bb, h, t: (0, h)),
        ],
        out_specs=pl.BlockSpec((1, GLA_T, GLA_HEAD_V), lambda bb, h, t: (bb, t, h)),
        out_shape=jax.ShapeDtypeStruct((b, s, GLA_VALUE_WIDTH), BF16),
        scratch_shapes=[pltpu.VMEM((GLA_HEAD_V, GLA_HEAD_K), F32)],
        compiler_params=pltpu.CompilerParams(dimension_semantics=("arbitrary", "arbitrary", "arbitrary")),
        name="gla",
    )(proj, proj, proj, proj, lr, wa, ba, og)


def _outproj_kernel(a_ref, b_ref, x_ref, w_ref, o_ref):
    y = jnp.dot(a_ref[...], w_ref[:SB_WIDTH, :], preferred_element_type=F32)
    y = y + jnp.dot(b_ref[...], w_ref[SB_WIDTH:, :], preferred_element_type=F32)
    o_ref[...] = x_ref[...] + y


def _outproj(a, b, xf, w):
    m = xf.shape[0]
    return pl.pallas_call(
        _outproj_kernel,
        grid=(m // OUT_TM,),
        in_specs=[
            pl.BlockSpec((OUT_TM, SB_WIDTH), lambda i: (i, 0)),
            pl.BlockSpec((OUT_TM, GLA_VALUE_WIDTH), lambda i: (i, 0)),
            pl.BlockSpec((OUT_TM, D_MODEL), lambda i: (i, 0)),
            pl.BlockSpec((SB_WIDTH + GLA_VALUE_WIDTH, D_MODEL), lambda i: (0, 0)),
        ],
        out_specs=pl.BlockSpec((OUT_TM, D_MODEL), lambda i: (i, 0)),
        out_shape=jax.ShapeDtypeStruct((m, D_MODEL), F32),
        compiler_params=pltpu.CompilerParams(dimension_semantics=("arbitrary",)),
        name="outproj",
    )(a, b, xf, w)


def _sb_constants():
    r = lax.broadcasted_iota(jnp.int32, (SB_TQ, SB_TQ), 0)
    c = lax.broadcasted_iota(jnp.int32, (SB_TQ, SB_TQ), 1)
    u = jnp.where(r > c, -1.0, 0.0).astype(BF16)
    diag_bias = jnp.where(c < r, 0.0, MASK_BIAS).astype(F32)
    return u, jnp.stack([jnp.zeros_like(diag_bias), diag_bias])


def _sb_schedule(s):
    n_steps = s // (SB_QB * SB_TQ)
    assert s // SB_TQ <= SB_TBL_JJ
    width = SB_QB * (n_steps - 1) * SB_QB + (SB_QB * (SB_QB + 1)) // 2
    rows = []
    for g in range(n_steps):
        row = [qb * SB_TBL_JJ + jj for qb in range(SB_QB) for jj in range(g * SB_QB + qb + 1)]
        rows.append(row + [0] * (width - len(row)))
    return jnp.asarray(rows, dtype=jnp.int32)


@jax.jit
def kernel(x, norm_gain, w_in, sb_q_gain, sb_k_gain, sb_o_gain, gla_w_alpha, gla_b_alpha, gla_o_gain, w_out):
    b, s, d = x.shape
    depth = w_in.shape[0]
    assert d == D_MODEL and w_in.shape[2] == MAIN_WIDTH + GLA_GATE_RANK
    assert s % IN_TM == 0 and s % GLA_T == 0 and s % (SB_QB * SB_TQ) == 0

    xf = x.reshape(b * s, d)
    u, bias = _sb_constants()
    tbl = _sb_schedule(s)
    q_scale = SB_HEAD_DIM ** -0.5 * LOG2E
    for layer in range(depth):
        w = w_in[layer]
        w_main = w[:, :MAIN_WIDTH].astype(BF16)
        w_lr = jnp.pad(w[:, MAIN_WIDTH:], ((0, 0), (0, LANES - GLA_GATE_RANK))).astype(BF16)
        hg = jnp.concatenate([jnp.tile(sb_q_gain[layer] * q_scale, SB_HEADS),
                              jnp.tile(sb_k_gain[layer], SB_HEADS)]).reshape(1, 2 * SB_WIDTH)
        proj, lr = _inproj(xf, norm_gain[layer].reshape(1, d), w_main, w_lr, hg, b, s)
        lr = lr.reshape(b, s, LANES)

        mixed_a = _sb_attention(proj, sb_o_gain[layer].reshape(SB_HEADS, 1, SB_HEAD_DIM), u, bias, tbl)

        wa = jnp.pad(gla_w_alpha[layer], ((0, LANES - GLA_GATE_RANK), (0, 0))).astype(BF16)
        mixed_b = _gla(proj, lr, wa, gla_b_alpha[layer].reshape(1, GLA_KEY_WIDTH),
                       gla_o_gain[layer].reshape(1, GLA_VALUE_WIDTH))

        xf = _outproj(mixed_a.reshape(b * s, SB_WIDTH), mixed_b.reshape(b * s, GLA_VALUE_WIDTH),
                      xf, w_out[layer].astype(BF16))
    return xf.reshape(b, s, d)
```

```python
import math

import jax
import jax.numpy as jnp
from jax import lax
from jax.experimental import pallas as pl
from jax.experimental.pallas import tpu as pltpu

F32 = jnp.float32
BF16 = jnp.bfloat16

D_MODEL = 2048
SB_WIDTH = 1024
SB_HEADS = 8
SB_HEAD_DIM = 128
GLA_HEADS = 4
GLA_KEY_WIDTH = 512
GLA_VALUE_WIDTH = 1024
GLA_HEAD_K = 128
GLA_HEAD_V = 256
GLA_GATE_RANK = 16
GLA_GATE_TAU = 16.0
GLA_CHUNK = 64
NORM_EPS = 1e-6
MAIN_WIDTH = 4 * SB_WIDTH + 2 * GLA_KEY_WIDTH + 2 * GLA_VALUE_WIDTH

LANES = 128
N_SLABS = MAIN_WIDTH // LANES
LOG2E = math.log2(math.e)
MASK_BIAS = -1e30
SB_Z_CAP = 126.0

SB_Q_SLAB, SB_K_SLAB, SB_V_SLAB, SB_G_SLAB = 0, 8, 16, 24
GLA_Q_SLAB, GLA_K_SLAB, GLA_V_SLAB, GLA_G_SLAB = 32, 36, 40, 48

IN_TM, IN_TN, IN_RB = 1024, 1024, 128
IN_MB = 128
IN_SLABS = IN_TN // LANES
IN_QK_BLOCKS = 2 * SB_WIDTH // IN_TN
SB_TQ = 256
SB_HP = 4
SB_QB = 4
SB_UNROLL = 4
SB_TBL_JJ = 16
SB_TBL_SHIFT = SB_TBL_JJ.bit_length() - 1
GLA_T = 1024
OUT_TM = 512


def _inproj_kernel(x_ref, ng_ref, w_ref, wlr_ref, wa_ref, ba_ref, hg_ref, o_ref, la_ref, h_scr):
    j = pl.program_id(1)

    @pl.when(j == 0)
    def _():
        def rows(r, c):
            sl = pl.ds(pl.multiple_of(r * IN_RB, IN_RB), IN_RB)
            x = x_ref[sl, :]
            ms = jnp.mean(x * x, axis=-1, keepdims=True)
            h_scr[sl, :] = (x * lax.rsqrt(ms + NORM_EPS) * ng_ref[...]).astype(BF16)
            return c

        lax.fori_loop(0, IN_TM // IN_RB, rows, 0)
        lr = jnp.dot(h_scr[...], wlr_ref[...], preferred_element_type=F32)
        logits = jnp.dot(lr.astype(BF16), wa_ref[...], preferred_element_type=F32) + ba_ref[...]
        la_ref[...] = (jnp.minimum(logits, 0.0) - jnp.log(1.0 + jnp.exp(-jnp.abs(logits)))) * (1.0 / GLA_GATE_TAU)

    is_qk = j < IN_QK_BLOCKS
    for r in range(IN_TM // IN_MB):
        rs = slice(r * IN_MB, (r + 1) * IN_MB)
        acc = jnp.dot(h_scr[rs, :], w_ref[...], preferred_element_type=F32)
        for hh in range(IN_SLABS):
            cs = slice(hh * LANES, (hh + 1) * LANES)
            a = acc[:, cs]
            ms = jnp.mean(a * a, axis=-1, keepdims=True)
            scale = jnp.where(is_qk, lax.rsqrt(ms + NORM_EPS), 1.0)
            o_ref[0, hh, rs, :] = (a * scale * hg_ref[:, cs]).astype(BF16)


def _inproj(xf, ng, w_all, layer, w_lr, wa, ba, hg, b, s):
    m = xf.shape[0]
    per_b = s // IN_TM
    return pl.pallas_call(
        _inproj_kernel,
        grid=(m // IN_TM, MAIN_WIDTH // IN_TN),
        in_specs=[
            pl.BlockSpec((IN_TM, D_MODEL), lambda i, j: (i, 0)),
            pl.BlockSpec((1, D_MODEL), lambda i, j: (0, 0)),
            pl.BlockSpec((None, D_MODEL, IN_TN), lambda i, j: (layer, 0, j)),
            pl.BlockSpec((D_MODEL, LANES), lambda i, j: (0, 0)),
            pl.BlockSpec((LANES, GLA_KEY_WIDTH), lambda i, j: (0, 0)),
            pl.BlockSpec((1, GLA_KEY_WIDTH), lambda i, j: (0, 0)),
            pl.BlockSpec((1, IN_TN), lambda i, j: (0, j)),
        ],
        out_specs=[
            pl.BlockSpec((1, IN_SLABS, IN_TM, LANES), lambda i, j: (i // per_b, j, i % per_b, 0)),
            pl.BlockSpec((IN_TM, GLA_KEY_WIDTH), lambda i, j: (i, 0)),
        ],
        out_shape=[
            jax.ShapeDtypeStruct((b, N_SLABS, s, LANES), BF16),
            jax.ShapeDtypeStruct((m, GLA_KEY_WIDTH), F32),
        ],
        scratch_shapes=[pltpu.VMEM((IN_TM, D_MODEL), BF16)],
        compiler_params=pltpu.CompilerParams(dimension_semantics=("arbitrary", "arbitrary")),
        name="inproj",
    )(xf, ng, w_all, w_lr, wa, ba, hg)


def _sb_kernel(tbl_ref, q_ref, k_ref, v_ref, g_ref, og_ref, u_ref, bias_ref, o_ref,
               acc_ref, l_ref, czs_buf, tot_buf):
    g = pl.program_id(2)
    first_qb = g * SB_QB
    n_pairs = SB_QB * first_qb + (SB_QB * (SB_QB + 1)) // 2
    spare = SB_QB * SB_HP

    @pl.when((pl.program_id(0) == 0) & (pl.program_id(1) == 0) & (g == 0))
    def _():
        czs_buf[...] = jnp.zeros_like(czs_buf)
        tot_buf[...] = jnp.zeros_like(tot_buf)

    acc_ref[...] = jnp.zeros_like(acc_ref)
    l_ref[...] = jnp.zeros_like(l_ref)
    u = u_ref[...]

    def pair(t):
        e = tbl_ref[g, jnp.clip(t, 0, n_pairs - 1)]
        qb = e >> SB_TBL_SHIFT
        jj = e & (SB_TBL_JJ - 1)
        key_off = pl.multiple_of((first_qb + qb - jj) * SB_TQ, SB_TQ)
        return qb, jj, key_off

    def step(t, slot_a):
        slot_b = 1 - slot_a

        qb3, _, key_off3 = pair(t - 1)
        valid3 = (t >= 1) & (t - 1 < n_pairs)
        for hp in range(SB_HP):
            idx = jnp.where(valid3, qb3 * SB_HP + hp, spare)
            czs = czs_buf[slot_b, hp]
            l = l_ref[idx]
            w = jnp.exp2(czs + jnp.concatenate([l, l], axis=1)).astype(BF16)
            vj = v_ref[0, hp, pl.ds(key_off3, SB_TQ), :]
            acc_ref[idx] += jnp.dot(w, vj, preferred_element_type=F32)
            l_ref[idx] = l + tot_buf[slot_b, hp]

        qb1, jj1, key_off1 = pair(t)
        q_off = pl.multiple_of(qb1 * SB_TQ, SB_TQ)
        bias = bias_ref[jnp.where(jj1 == 0, 1, 0)]
        for hp in range(SB_HP):
            z = lax.dot_general(
                q_ref[0, hp, pl.ds(q_off, SB_TQ), :], k_ref[0, hp, pl.ds(key_off1, SB_TQ), :],
                (((1,), (1,)), ((), ())), preferred_element_type=F32) + bias
            sp = jnp.maximum(z, jnp.log(1.0 + jnp.exp2(jnp.minimum(z, SB_Z_CAP))) * LOG2E)
            cum = jnp.dot(sp.astype(BF16), u, preferred_element_type=F32)
            czs_buf[slot_a, hp] = cum + (z - sp)
            tot_buf[slot_a, hp] = jnp.broadcast_to(cum[:, :1] - sp[:, :1], (SB_TQ, LANES))

    def body(m, c):
        for k in range(SB_UNROLL):
            step(SB_UNROLL * m + k, k % 2)
        return c

    lax.fori_loop(0, (n_pairs + 2) // SB_UNROLL, body, 0)

    for qb in range(SB_QB):
        rows = slice(qb * SB_TQ, (qb + 1) * SB_TQ)
        for hp in range(SB_HP):
            o = acc_ref[qb * SB_HP + hp]
            ms = jnp.mean(o * o, axis=-1, keepdims=True)
            y = o * lax.rsqrt(ms + NORM_EPS) * og_ref[hp]
            gate = g_ref[0, hp, rows, :].astype(F32)
            o_ref[0, rows, hp * LANES:(hp + 1) * LANES] = (y * (gate / (1.0 + jnp.exp(-gate)))).astype(BF16)


def _sb_attention(proj, og, u, bias, tbl):
    b, _, s, _ = proj.shape
    tile = (SB_TQ, SB_TQ)
    rows = SB_QB * SB_TQ
    return pl.pallas_call(
        _sb_kernel,
        grid=(b, SB_HEADS // SB_HP, s // rows),
        in_specs=[
            pl.BlockSpec(memory_space=pltpu.SMEM),
            pl.BlockSpec((1, SB_HP, rows, LANES), lambda bb, h, i: (bb, SB_Q_SLAB // SB_HP + h, i, 0)),
            pl.BlockSpec((1, SB_HP, s, LANES), lambda bb, h, i: (bb, SB_K_SLAB // SB_HP + h, 0, 0)),
            pl.BlockSpec((1, SB_HP, s, LANES), lambda bb, h, i: (bb, SB_V_SLAB // SB_HP + h, 0, 0)),
            pl.BlockSpec((1, SB_HP, rows, LANES), lambda bb, h, i: (bb, SB_G_SLAB // SB_HP + h, i, 0)),
            pl.BlockSpec((SB_HP, 1, LANES), lambda bb, h, i: (h, 0, 0)),
            pl.BlockSpec(tile, lambda bb, h, i: (0, 0)),
            pl.BlockSpec((2,) + tile, lambda bb, h, i: (0, 0, 0)),
        ],
        out_specs=pl.BlockSpec((1, rows, SB_HP * LANES), lambda bb, h, i: (bb, i, h)),
        out_shape=jax.ShapeDtypeStruct((b, s, SB_WIDTH), BF16),
        scratch_shapes=[
            pltpu.VMEM((SB_QB * SB_HP + 1, SB_TQ, SB_HEAD_DIM), F32),
            pltpu.VMEM((SB_QB * SB_HP + 1, SB_TQ, LANES), F32),
            pltpu.VMEM((2, SB_HP) + tile, F32),
            pltpu.VMEM((2, SB_HP, SB_TQ, LANES), F32),
        ],
        compiler_params=pltpu.CompilerParams(dimension_semantics=("arbitrary", "arbitrary", "arbitrary")),
        name="sb_attention",
    )(tbl, proj, proj, proj, proj, og, u, bias)


def _gla_kernel(q_ref, k_ref, v_ref, g_ref, la_ref, og_ref, o_ref, st_ref):
    t = pl.program_id(2)
    c_len = GLA_CHUNK

    @pl.when(t == 0)
    def _():
        st_ref[...] = jnp.zeros_like(st_ref)

    la = la_ref[0]

    row = lax.broadcasted_iota(jnp.int32, (c_len, c_len), 0)
    col = lax.broadcasted_iota(jnp.int32, (c_len, c_len), 1)
    causal = col <= row
    ltri = jnp.where(causal, 1.0, 0.0).astype(BF16)
    ltri2 = jnp.concatenate([ltri, ltri], axis=1)
    scale = GLA_HEAD_K ** -0.5
    st = st_ref[...]

    chunks = [slice(c * c_len, (c + 1) * c_len) for c in range(GLA_T // c_len)]
    vbs = [jnp.concatenate([v_ref[0, 0, sl, :], v_ref[0, 1, sl, :]], axis=1) for sl in chunks]
    vts = [vb.T for vb in vbs]

    la_hi = la.astype(BF16)
    la_lo = (la - la_hi.astype(F32)).astype(BF16)
    cums = [jnp.dot(ltri2, jnp.concatenate([la_hi[sl], la_lo[sl]], axis=0), preferred_element_type=F32)
            for sl in chunks]
    cls = [cum[c_len - 1:c_len, :] for cum in cums]

    qds, kis, kls = [], [], []
    for sl, cum, cl in zip(chunks, cums, cls):
        qf = q_ref[0, 0, sl, :].astype(F32) * scale
        kf = k_ref[0, 0, sl, :].astype(F32)
        qds.append((qf * jnp.exp(cum)).astype(BF16))
        kis.append((kf * jnp.exp(-cum)).astype(BF16))
        kls.append((kf * jnp.exp(cl - cum)).astype(BF16))

    kvts = [jnp.dot(vt, kl, preferred_element_type=F32) for vt, kl in zip(vts, kls)]
    o_intra = []
    for qd, ki, vb in zip(qds, kis, vbs):
        sc = lax.dot_general(qd, ki, (((1,), (1,)), ((), ())), preferred_element_type=F32)
        o_intra.append(jnp.dot(jnp.where(causal, sc, 0.0).astype(BF16), vb, preferred_element_type=F32))

    states = []
    for cl, kvt in zip(cls, kvts):
        states.append(st.astype(BF16))
        st = jnp.exp(cl) * st + kvt
    st_ref[...] = st

    for sl, qd, oi, stb in zip(chunks, qds, o_intra, states):
        o = oi + lax.dot_general(qd, stb, (((1,), (1,)), ((), ())), preferred_element_type=F32)
        ms = jnp.mean(o * o, axis=-1, keepdims=True)
        y = o * lax.rsqrt(ms + NORM_EPS) * og_ref[...]
        g = jnp.concatenate([g_ref[0, 0, sl, :], g_ref[0, 1, sl, :]], axis=1).astype(F32)
        o_ref[0, sl, :] = (y * (g / (1.0 + jnp.exp(-g)))).astype(BF16)


def _gla(proj, la, og):
    b, _, s, _ = proj.shape
    return pl.pallas_call(
        _gla_kernel,
        grid=(b, GLA_HEADS, s // GLA_T),
        in_specs=[
            pl.BlockSpec((1, 1, GLA_T, LANES), lambda bb, h, t: (bb, GLA_Q_SLAB + h, t, 0)),
            pl.BlockSpec((1, 1, GLA_T, LANES), lambda bb, h, t: (bb, GLA_K_SLAB + h, t, 0)),
            pl.BlockSpec((1, 2, GLA_T, LANES), lambda bb, h, t: (bb, GLA_V_SLAB // 2 + h, t, 0)),
            pl.BlockSpec((1, 2, GLA_T, LANES), lambda bb, h, t: (bb, GLA_G_SLAB // 2 + h, t, 0)),
            pl.BlockSpec((1, GLA_T, GLA_HEAD_K), lambda bb, h, t: (bb, t, h)),
            pl.BlockSpec((1, GLA_HEAD_V), lambda bb, h, t: (0, h)),
        ],
        out_specs=pl.BlockSpec((1, GLA_T, GLA_HEAD_V), lambda bb, h, t: (bb, t, h)),
        out_shape=jax.ShapeDtypeStruct((b, s, GLA_VALUE_WIDTH), BF16),
        scratch_shapes=[pltpu.VMEM((GLA_HEAD_V, GLA_HEAD_K), F32)],
        compiler_params=pltpu.CompilerParams(dimension_semantics=("arbitrary", "arbitrary", "arbitrary")),
        name="gla",
    )(proj, proj, proj, proj, la, og)


def _outproj_kernel(a_ref, b_ref, x_ref, w_ref, o_ref):
    y = jnp.dot(a_ref[...], w_ref[:SB_WIDTH, :], preferred_element_type=F32)
    y = y + jnp.dot(b_ref[...], w_ref[SB_WIDTH:, :], preferred_element_type=F32)
    o_ref[...] = x_ref[...] + y


def _outproj(a, b, xf, w_all, layer):
    m = xf.shape[0]
    return pl.pallas_call(
        _outproj_kernel,
        grid=(m // OUT_TM,),
        in_specs=[
            pl.BlockSpec((OUT_TM, SB_WIDTH), lambda i: (i, 0)),
            pl.BlockSpec((OUT_TM, GLA_VALUE_WIDTH), lambda i: (i, 0)),
            pl.BlockSpec((OUT_TM, D_MODEL), lambda i: (i, 0)),
            pl.BlockSpec((None, SB_WIDTH + GLA_VALUE_WIDTH, D_MODEL), lambda i: (layer, 0, 0)),
        ],
        out_specs=pl.BlockSpec((OUT_TM, D_MODEL), lambda i: (i, 0)),
        out_shape=jax.ShapeDtypeStruct((m, D_MODEL), F32),
        compiler_params=pltpu.CompilerParams(dimension_semantics=("arbitrary",)),
        name="outproj",
    )(a, b, xf, w_all)


def _sb_constants():
    r = lax.broadcasted_iota(jnp.int32, (SB_TQ, SB_TQ), 0)
    c = lax.broadcasted_iota(jnp.int32, (SB_TQ, SB_TQ), 1)
    u = jnp.where(r > c, -1.0, 0.0).astype(BF16)
    diag_bias = jnp.where(c < r, 0.0, MASK_BIAS).astype(F32)
    return u, jnp.stack([jnp.zeros_like(diag_bias), diag_bias])


def _sb_schedule(s):
    n_steps = s // (SB_QB * SB_TQ)
    assert s // SB_TQ <= SB_TBL_JJ
    width = SB_QB * (n_steps - 1) * SB_QB + (SB_QB * (SB_QB + 1)) // 2
    rows = []
    for g in range(n_steps):
        row = [qb * SB_TBL_JJ + jj for qb in range(SB_QB) for jj in range(g * SB_QB + qb + 1)]
        rows.append(row + [0] * (width - len(row)))
    return jnp.asarray(rows, dtype=jnp.int32)


@jax.jit
def kernel(x, norm_gain, w_in, sb_q_gain, sb_k_gain, sb_o_gain, gla_w_alpha, gla_b_alpha, gla_o_gain, w_out):
    b, s, d = x.shape
    depth = w_in.shape[0]
    assert d == D_MODEL and w_in.shape[2] == MAIN_WIDTH + GLA_GATE_RANK
    assert s % IN_TM == 0 and s % GLA_T == 0 and s % (SB_QB * SB_TQ) == 0

    xf = x.reshape(b * s, d)
    u, bias = _sb_constants()
    tbl = _sb_schedule(s)
    q_scale = SB_HEAD_DIM ** -0.5 * LOG2E
    w_in_bf = w_in.astype(BF16)
    w_out_bf = w_out.astype(BF16)
    for layer in range(depth):
        w_lr = jnp.pad(w_in_bf[layer, :, MAIN_WIDTH:], ((0, 0), (0, LANES - GLA_GATE_RANK)))
        hg = jnp.concatenate([jnp.tile(sb_q_gain[layer] * q_scale, SB_HEADS),
                              jnp.tile(sb_k_gain[layer], SB_HEADS),
                              jnp.ones((MAIN_WIDTH - 2 * SB_WIDTH,), F32)]).reshape(1, MAIN_WIDTH)
        wa = jnp.pad(gla_w_alpha[layer], ((0, LANES - GLA_GATE_RANK), (0, 0))).astype(BF16)
        proj, la = _inproj(xf, norm_gain[layer].reshape(1, d), w_in_bf, layer, w_lr, wa,
                           gla_b_alpha[layer].reshape(1, GLA_KEY_WIDTH), hg, b, s)
        la = la.reshape(b, s, GLA_KEY_WIDTH)

        mixed_a = _sb_attention(proj, sb_o_gain[layer].reshape(SB_HEADS, 1, SB_HEAD_DIM), u, bias, tbl)

        mixed_b = _gla(proj, la, gla_o_gain[layer].reshape(1, GLA_VALUE_WIDTH))

        xf = _outproj(mixed_a.reshape(b * s, SB_WIDTH), mixed_b.reshape(b * s, GLA_VALUE_WIDTH),
                      xf, w_out_bf, layer)
    return xf.reshape(b, s, d)
```
